```python
import jax, jax.numpy as jnp
from jax import lax
import numpy as np

D_MODEL = 1024
BATCH = 4
SEQ = 4096
DEPTH = 4
DEC_BATCH = 4
DEC_SEQ = 8192
PAST_LEN = 128

CONV_WIDTH = D_MODEL
CONV_KERNEL = 31
CONV_PAD = CONV_KERNEL // 2
SGU_WIDTH = D_MODEL
SGU_GROUPS = 8
SGU_GROUP_DIM = SGU_WIDTH // SGU_GROUPS
CHUNK = 128
RMS_EPS = 1e-6
LN_EPS = 1e-5

IN_COLS_LIST = [CONV_WIDTH, CONV_WIDTH, CONV_WIDTH, SGU_WIDTH, SGU_WIDTH, SGU_WIDTH, D_MODEL, D_MODEL]
IN_COLS = int(sum(IN_COLS_LIST))
IN_SPLITS = [int(v) for v in np.cumsum(IN_COLS_LIST)[:-1]]

kernel_name = "hybrid_conv_sgu_gated_encoder"


def rms_norm(x, g):
    xf = x.astype(jnp.float32)
    y = xf * lax.rsqrt(jnp.mean(xf * xf, axis=-1, keepdims=True) + RMS_EPS)
    return (y * g.astype(jnp.float32)).astype(x.dtype)


def layer_norm(x, g, b):
    xf = x.astype(jnp.float32)
    mu = jnp.mean(xf, axis=-1, keepdims=True)
    xc = xf - mu
    var = jnp.mean(xc * xc, axis=-1, keepdims=True)
    y = xc * lax.rsqrt(var + LN_EPS) * g.astype(jnp.float32) + b.astype(jnp.float32)
    return y.astype(x.dtype)


def conv_branch(a_val, a_glu, a_z, conv_w, conv_b, ln_g, ln_b, w_proj):
    h = a_val * jax.nn.sigmoid(a_glu)
    h = lax.conv_general_dilated(
        h, conv_w[:, None, :].astype(h.dtype),
        window_strides=(1,), padding=[(CONV_PAD, CONV_PAD)],
        dimension_numbers=("NWC", "WIO", "NWC"),
        feature_group_count=CONV_WIDTH) + conv_b
    h = layer_norm(h, ln_g, ln_b)
    h = jax.nn.silu(h) * jax.nn.silu(a_z)
    return h @ w_proj


def sgu_branch(u, v, b_z, ln_g, ln_b, w_s, b_s, w_proj):
    u = jax.nn.gelu(u)
    v = layer_norm(jax.nn.gelu(v), ln_g, ln_b)
    bsz, seq, _ = v.shape
    vc = v.reshape(bsz, seq // CHUNK, CHUNK, SGU_GROUPS, SGU_GROUP_DIM)
    mixed = jnp.einsum("gpq,bnqgc->bnpgc", w_s, vc) + jnp.transpose(b_s)[None, None, :, :, None]
    h = u * mixed.reshape(bsz, seq, SGU_WIDTH) * jax.nn.silu(b_z)
    return h @ w_proj


def encoder_layer(x, c, w_ada, b_ada, g_pre, w_in, conv_w, conv_b, conv_ln_g, conv_ln_b, conv_proj,
                  sgu_ln_g, sgu_ln_b, sgu_ws, sgu_bs, sgu_proj, w_out, g_post):
    mod = jax.nn.silu(c) @ w_ada + b_ada
    shift, scale, gate = jnp.split(mod[:, None, :], 3, axis=-1)
    h = rms_norm(x, g_pre) * (1 + scale) + shift
    p = h @ w_in
    a_val, a_glu, a_z, u, v, b_z, g_a, g_b = jnp.split(p, IN_SPLITS, axis=-1)
    y_a = conv_branch(a_val, a_glu, a_z, conv_w, conv_b, conv_ln_g, conv_ln_b, conv_proj)
    y_b = sgu_branch(u, v, b_z, sgu_ln_g, sgu_ln_b, sgu_ws, sgu_bs, sgu_proj)
    m = jax.nn.sigmoid(g_a) * y_a + jax.nn.sigmoid(g_b) * y_b
    o = m @ w_out
    return x + gate * rms_norm(o, g_post)


def trunk(x, c, w_ada, b_ada, g_pre, w_in, conv_w, conv_b, conv_ln_g, conv_ln_b, conv_proj,
          sgu_ln_g, sgu_ln_b, sgu_ws, sgu_bs, sgu_proj, w_out, g_post):
    for l in range(DEPTH):
        x = encoder_layer(x, c, w_ada[l], b_ada[l], g_pre[l], w_in[l], conv_w[l], conv_b[l],
                          conv_ln_g[l], conv_ln_b[l], conv_proj[l], sgu_ln_g[l], sgu_ln_b[l],
                          sgu_ws[l], sgu_bs[l], sgu_proj[l], w_out[l], g_post[l])
    return x


def setup_inputs(seed: int = 0) -> dict:
    key = jax.random.key(seed)
    ks = jax.random.split(key, 24)
    f32 = jnp.float32
    nrm = lambda k, shape, s: jax.random.normal(k, shape, f32) * s
    D = D_MODEL
    return {
        "x_prompt": nrm(ks[0], (BATCH, SEQ, D), 1.0),
        "x_sample": nrm(ks[1], (DEC_BATCH, DEC_SEQ, D), 1.0),
        "c_prompt": nrm(ks[2], (BATCH, D), 1.0),
        "c_sample": nrm(ks[3], (DEC_BATCH, D), 1.0),
        "w_ada": nrm(ks[4], (DEPTH, D, 3 * D), 0.5 * D ** -0.5),
        "b_ada": nrm(ks[5], (DEPTH, 3 * D), 0.02),
        "g_pre": 1.0 + nrm(ks[6], (DEPTH, D), 0.02),
        "w_in": nrm(ks[7], (DEPTH, D, IN_COLS), D ** -0.5),
        "conv_w": nrm(ks[8], (DEPTH, CONV_KERNEL, CONV_WIDTH), CONV_KERNEL ** -0.5),
        "conv_b": nrm(ks[9], (DEPTH, CONV_WIDTH), 0.02),
        "conv_ln_g": 1.0 + nrm(ks[10], (DEPTH, CONV_WIDTH), 0.02),
        "conv_ln_b": nrm(ks[11], (DEPTH, CONV_WIDTH), 0.02),
        "conv_proj": nrm(ks[12], (DEPTH, CONV_WIDTH, D), CONV_WIDTH ** -0.5),
        "sgu_ln_g": 1.0 + nrm(ks[13], (DEPTH, SGU_WIDTH), 0.02),
        "sgu_ln_b": nrm(ks[14], (DEPTH, SGU_WIDTH), 0.02),
        "sgu_ws": nrm(ks[15], (DEPTH, SGU_GROUPS, CHUNK, CHUNK), CHUNK ** -0.5),
        "sgu_bs": 1.0 + nrm(ks[16], (DEPTH, SGU_GROUPS, CHUNK), 0.02),
        "sgu_proj": nrm(ks[17], (DEPTH, SGU_WIDTH, D), SGU_WIDTH ** -0.5),
        "w_out": nrm(ks[18], (DEPTH, D, D), D ** -0.5),
        "g_post": 1.0 + nrm(ks[19], (DEPTH, D), 0.02),
    }


def reference(x_prompt, x_sample, c_prompt, c_sample, w_ada, b_ada, g_pre, w_in, conv_w, conv_b,
              conv_ln_g, conv_ln_b, conv_proj, sgu_ln_g, sgu_ln_b, sgu_ws, sgu_bs, sgu_proj, w_out, g_post):
    y_prompt = trunk(x_prompt, c_prompt, w_ada, b_ada, g_pre, w_in, conv_w, conv_b, conv_ln_g, conv_ln_b,
                     conv_proj, sgu_ln_g, sgu_ln_b, sgu_ws, sgu_bs, sgu_proj, w_out, g_post)
    y_sample = trunk(x_sample, c_sample, w_ada, b_ada, g_pre, w_in, conv_w, conv_b, conv_ln_g, conv_ln_b,
                     conv_proj, sgu_ln_g, sgu_ln_b, sgu_ws, sgu_bs, sgu_proj, w_out, g_post)
    return (y_prompt, y_sample)
```

```python
import functools

import jax
import jax.numpy as jnp
from jax import lax
from jax.experimental import pallas as pl
from jax.experimental.pallas import tpu as pltpu

D_MODEL = 1024
CONV_KERNEL = 31
CONV_PAD = CONV_KERNEL // 2
SGU_GROUPS = 8
GROUP_DIM = D_MODEL // SGU_GROUPS
CHUNK = 128
RMS_EPS = 1e-6
LN_EPS = 1e-5

HALO = 16
ROW_TILE = 512
CONV_ROWS = 64
LANES = 128
VMEM_LIMIT_BYTES = 56 * 1024 * 1024


def _mod_kernel(c_ref, w_ref, b_ref, o_ref):
    c = c_ref[...]
    o_ref[...] = jnp.dot(jax.nn.silu(c), w_ref[...], preferred_element_type=jnp.float32) + b_ref[...]


def _modulation(c_all, w_ada, b_ada):
    depth, d, d3 = w_ada.shape
    r = c_all.shape[0]
    n_col = d3 // d
    return pl.pallas_call(
        _mod_kernel,
        grid=(depth, n_col),
        in_specs=[
            pl.BlockSpec((r, d), lambda l, j: (0, 0)),
            pl.BlockSpec((None, d, d), lambda l, j: (l, 0, j)),
            pl.BlockSpec((None, 1, d), lambda l, j: (l, 0, j)),
        ],
        out_specs=pl.BlockSpec((None, r, d), lambda l, j: (l, 0, j)),
        out_shape=jax.ShapeDtypeStruct((depth, r, d3), jnp.float32),
        name="adaln_mod",
    )(c_all, w_ada, b_ada.reshape(depth, 1, d3))


def _layer_norm(x, g, b):
    mu = jnp.mean(x, axis=-1, keepdims=True)
    xc = x - mu
    var = jnp.mean(xc * xc, axis=-1, keepdims=True)
    return xc * lax.rsqrt(var + LN_EPS) * g + b


def _layer_kernel(x_ref, xp_ref, xn_ref, mod_ref, gpre_ref, win_ref, convw_ref, convb_ref,
                  clng_ref, clnb_ref, cproj_ref, slng_ref, slnb_ref, ws_ref, bs_ref,
                  sproj_ref, wout_ref, gpost_ref, o_ref, h_ref, hs_ref, conv_ref):
    tm = x_ref.shape[0]
    d = D_MODEL
    s_idx = pl.program_id(1)
    n_s = pl.num_programs(1)

    shift = mod_ref[0:1, :]
    scale1 = 1.0 + mod_ref[1:2, :]
    gate = mod_ref[2:3, :]
    g_pre = gpre_ref[...]

    def pre(xv):
        y = xv * lax.rsqrt(jnp.mean(xv * xv, axis=-1, keepdims=True) + RMS_EPS)
        return ((y * g_pre) * scale1 + shift).astype(jnp.bfloat16)

    h_ref[0:HALO, :] = pre(xp_ref[...])
    h_ref[HALO:HALO + tm, :] = pre(x_ref[...])
    h_ref[HALO + tm:, :] = pre(xn_ref[...])

    def proj(col, rows=slice(HALO, HALO + ROW_TILE)):
        return jnp.dot(h_ref[rows, :], win_ref[:, col * d:(col + 1) * d],
                       preferred_element_type=jnp.float32)

    all_rows = slice(0, tm + 2 * HALO)
    hg = proj(0, all_rows) * jax.nn.sigmoid(proj(1, all_rows))
    row = lax.broadcasted_iota(jnp.int32, (tm + 2 * HALO, 1), 0)
    valid = jnp.logical_and(jnp.logical_or(row >= HALO, s_idx > 0),
                            jnp.logical_or(row < HALO + tm, s_idx < n_s - 1))
    hs_ref[...] = jnp.where(valid, hg, 0.0)

    def conv_block(lb, carry):
        lanes = pl.ds(pl.multiple_of(lb * LANES, LANES), LANES)
        for r0 in range(0, tm, CONV_ROWS):
            acc = jnp.zeros((CONV_ROWS, LANES), jnp.float32)
            for k in range(CONV_KERNEL):
                start = r0 + HALO - CONV_PAD + k
                acc = acc + convw_ref[k:k + 1, lanes] * hs_ref[start:start + CONV_ROWS, lanes]
            conv_ref[r0:r0 + CONV_ROWS, lanes] = acc
        return carry

    lax.fori_loop(0, d // LANES, conv_block, 0)

    hc = conv_ref[...] + convb_ref[...]
    hc = _layer_norm(hc, clng_ref[...], clnb_ref[...])
    hc = jax.nn.silu(hc) * jax.nn.silu(proj(2))
    y_a = jnp.dot(hc.astype(jnp.bfloat16), cproj_ref[...], preferred_element_type=jnp.float32)
    m = jax.nn.sigmoid(proj(6)) * y_a

    v = _layer_norm(jax.nn.gelu(proj(4)), slng_ref[...], slnb_ref[...]).astype(jnp.bfloat16)
    mixed_rows = []
    for n in range(tm // CHUNK):
        cols = []
        for g in range(SGU_GROUPS):
            vb = v[n * CHUNK:(n + 1) * CHUNK, g * GROUP_DIM:(g + 1) * GROUP_DIM]
            mg = jnp.dot(ws_ref[g], vb, preferred_element_type=jnp.float32)
            cols.append(mg + bs_ref[g])
        mixed_rows.append(jnp.concatenate(cols, axis=1))
    mixed = jnp.concatenate(mixed_rows, axis=0)
    hb = jax.nn.gelu(proj(3)) * mixed * jax.nn.silu(proj(5))
    y_b = jnp.dot(hb.astype(jnp.bfloat16), sproj_ref[...], preferred_element_type=jnp.float32)
    m = m + jax.nn.sigmoid(proj(7)) * y_b

    o = jnp.dot(m.astype(jnp.bfloat16), wout_ref[...], preferred_element_type=jnp.float32)
    o = o * lax.rsqrt(jnp.mean(o * o, axis=-1, keepdims=True) + RMS_EPS) * gpost_ref[...]
    o_ref[...] = x_ref[...] + gate * o


def _const_spec(shape):
    zeros = (0,) * len(shape)
    return pl.BlockSpec(shape, lambda b, s: zeros, pipeline_mode=pl.Buffered(1))


def _encoder_layer(x, mod, g_pre, w_in, conv_w, conv_b, cln_g, cln_b, cproj, sln_g, sln_b,
                   ws, bs, sproj, w_out, g_post):
    bsz, seq, d = x.shape
    tm = ROW_TILE
    assert seq % tm == 0 and tm % CHUNK == 0 and d == D_MODEL
    n_s = seq // tm
    halo_per_tile = tm // HALO
    n_halo_blocks = seq // HALO

    in_specs = [
        pl.BlockSpec((None, tm, d), lambda b, s: (b, s, 0)),
        pl.BlockSpec((None, HALO, d), lambda b, s: (b, jnp.maximum(s * halo_per_tile - 1, 0), 0)),
        pl.BlockSpec((None, HALO, d),
                     lambda b, s: (b, jnp.minimum((s + 1) * halo_per_tile, n_halo_blocks - 1), 0)),
        pl.BlockSpec((None, 3, d), lambda b, s: (b, 0, 0)),
        _const_spec(g_pre.shape), _const_spec(w_in.shape), _const_spec(conv_w.shape),
        _const_spec(conv_b.shape), _const_spec(cln_g.shape), _const_spec(cln_b.shape),
        _const_spec(cproj.shape), _const_spec(sln_g.shape), _const_spec(sln_b.shape),
        _const_spec(ws.shape), _const_spec(bs.shape), _const_spec(sproj.shape),
        _const_spec(w_out.shape), _const_spec(g_post.shape),
    ]
    return pl.pallas_call(
        _layer_kernel,
        grid=(bsz, n_s),
        in_specs=in_specs,
        out_specs=pl.BlockSpec((None, tm, d), lambda b, s: (b, s, 0)),
        out_shape=jax.ShapeDtypeStruct(x.shape, x.dtype),
        scratch_shapes=[
            pltpu.VMEM((tm + 2 * HALO, d), jnp.bfloat16),
            pltpu.VMEM((tm + 2 * HALO, d), jnp.float32),
            pltpu.VMEM((tm, d), jnp.float32),
        ],
        compiler_params=pltpu.CompilerParams(
            dimension_semantics=("arbitrary", "arbitrary"),
            vmem_limit_bytes=VMEM_LIMIT_BYTES),
        name="encoder_layer",
    )(x, x, x, mod, g_pre, w_in, conv_w, conv_b, cln_g, cln_b, cproj, sln_g, sln_b,
      ws, bs, sproj, w_out, g_post)


def kernel(x_prompt, x_sample, c_prompt, c_sample, w_ada, b_ada, g_pre, w_in, conv_w, conv_b,
           conv_ln_g, conv_ln_b, conv_proj, sgu_ln_g, sgu_ln_b, sgu_ws, sgu_bs, sgu_proj, w_out,
           g_post):
    depth = w_ada.shape[0]
    d = D_MODEL
    bp = c_prompt.shape[0]
    mod = _modulation(jnp.concatenate([c_prompt, c_sample], axis=0), w_ada, b_ada)
    mod = mod.reshape(depth, mod.shape[1], 3, d)

    bf16 = jnp.bfloat16
    w_in_b = w_in.astype(bf16)
    cproj_b = conv_proj.astype(bf16)
    sproj_b = sgu_proj.astype(bf16)
    w_out_b = w_out.astype(bf16)
    ws_b = sgu_ws.astype(bf16)
    row = lambda a, l: a[l].reshape(1, d)

    xs = [x_prompt, x_sample]
    mods = [mod[:, :bp], mod[:, bp:]]
    for l in range(depth):
        for i in range(2):
            xs[i] = _encoder_layer(
                xs[i], mods[i][l], row(g_pre, l), w_in_b[l], conv_w[l], row(conv_b, l),
                row(conv_ln_g, l), row(conv_ln_b, l), cproj_b[l], row(sgu_ln_g, l),
                row(sgu_ln_b, l), ws_b[l], sgu_bs[l].reshape(SGU_GROUPS, CHUNK, 1), sproj_b[l],
                w_out_b[l], row(g_post, l))
    return (xs[0], xs[1])
```

```python
import functools

import jax
import jax.numpy as jnp
from jax import lax
from jax.experimental import pallas as pl
from jax.experimental.pallas import tpu as pltpu

D_MODEL = 1024
CONV_KERNEL = 31
CONV_PAD = CONV_KERNEL // 2
SGU_GROUPS = 8
GROUP_DIM = D_MODEL // SGU_GROUPS
CHUNK = 128
RMS_EPS = 1e-6
LN_EPS = 1e-5

HALO = 16
ROW_TILE = 512
LANES = 128
SUBLANES = 8
CONV_STRIDE = 4
CONV_ROWS = SUBLANES * CONV_STRIDE
VMEM_LIMIT_BYTES = 56 * 1024 * 1024


def _mod_kernel(c_ref, w_ref, b_ref, o_ref):
    c = c_ref[...]
    o_ref[...] = jnp.dot(jax.nn.silu(c), w_ref[...], preferred_element_type=jnp.float32) + b_ref[...]


def _modulation(c_all, w_ada, b_ada):
    depth, d, d3 = w_ada.shape
    r = c_all.shape[0]
    n_col = d3 // d
    return pl.pallas_call(
        _mod_kernel,
        grid=(depth, n_col),
        in_specs=[
            pl.BlockSpec((r, d), lambda l, j: (0, 0)),
            pl.BlockSpec((None, d, d), lambda l, j: (l, 0, j)),
            pl.BlockSpec((None, 1, d), lambda l, j: (l, 0, j)),
        ],
        out_specs=pl.BlockSpec((None, r, d), lambda l, j: (l, 0, j)),
        out_shape=jax.ShapeDtypeStruct((depth, r, d3), jnp.float32),
        name="adaln_mod",
    )(c_all, w_ada, b_ada.reshape(depth, 1, d3))


def _layer_norm(x, g, b):
    mu = jnp.mean(x, axis=-1, keepdims=True)
    xc = x - mu
    var = jnp.mean(xc * xc, axis=-1, keepdims=True)
    return xc * lax.rsqrt(var + LN_EPS) * g + b


def _layer_kernel(x_ref, xp_ref, xn_ref, mod_ref, gpre_ref, win_ref, convw_ref, convb_ref,
                  clng_ref, clnb_ref, cproj_ref, slng_ref, slnb_ref, ws_ref, bs_ref,
                  sproj_ref, wout_ref, gpost_ref, o_ref, h_ref, hs_ref, conv_ref):
    tm = x_ref.shape[0]
    d = D_MODEL
    s_idx = pl.program_id(1)
    n_s = pl.num_programs(1)

    shift = mod_ref[0:1, :]
    scale1 = 1.0 + mod_ref[1:2, :]
    gate = mod_ref[2:3, :]
    g_pre = gpre_ref[...]

    def pre(xv):
        y = xv * lax.rsqrt(jnp.mean(xv * xv, axis=-1, keepdims=True) + RMS_EPS)
        return ((y * g_pre) * scale1 + shift).astype(jnp.bfloat16)

    h_ref[0:HALO, :] = pre(xp_ref[...])
    h_ref[HALO:HALO + tm, :] = pre(x_ref[...])
    h_ref[HALO + tm:, :] = pre(xn_ref[...])

    def proj(col, rows=slice(HALO, HALO + ROW_TILE)):
        return jnp.dot(h_ref[rows, :], win_ref[:, col * d:(col + 1) * d],
                       preferred_element_type=jnp.float32)

    all_rows = slice(0, tm + 2 * HALO)
    hg = proj(0, all_rows) * jax.nn.sigmoid(proj(1, all_rows))
    row = lax.broadcasted_iota(jnp.int32, (tm + 2 * HALO, 1), 0)
    valid = jnp.logical_and(jnp.logical_or(row >= HALO, s_idx > 0),
                            jnp.logical_or(row < HALO + tm, s_idx < n_s - 1))
    hg = jnp.where(valid, hg, 0.0)
    n_lb = d // LANES
    for lb in range(n_lb):
        hs_ref[lb] = hg[:, lb * LANES:(lb + 1) * LANES]

    for lb in range(n_lb):
        w_taps = [jnp.broadcast_to(convw_ref[j:j + 1, lb * LANES:(lb + 1) * LANES], (SUBLANES, LANES))
                  for j in range(CONV_KERNEL)]
        for base in range(0, tm, CONV_ROWS):
            acc = [None] * CONV_STRIDE
            for off in range(CONV_STRIDE + CONV_KERNEL - 1):
                e = hs_ref[lb, pl.ds(base + HALO - CONV_PAD + off, SUBLANES, stride=CONV_STRIDE), :]
                for q in range(CONV_STRIDE):
                    j = off - q
                    if 0 <= j < CONV_KERNEL:
                        term = w_taps[j] * e
                        acc[q] = term if acc[q] is None else acc[q] + term
            for q in range(CONV_STRIDE):
                conv_ref[lb, pl.ds(base + q, SUBLANES, stride=CONV_STRIDE), :] = acc[q]

    hc = jnp.concatenate([conv_ref[lb] for lb in range(n_lb)], axis=1) + convb_ref[...]
    hc = _layer_norm(hc, clng_ref[...], clnb_ref[...])
    hc = jax.nn.silu(hc) * jax.nn.silu(proj(2))
    y_a = jnp.dot(hc.astype(jnp.bfloat16), cproj_ref[...], preferred_element_type=jnp.float32)
    m = jax.nn.sigmoid(proj(6)) * y_a

    v = _layer_norm(jax.nn.gelu(proj(4)), slng_ref[...], slnb_ref[...]).astype(jnp.bfloat16)
    mixed_rows = []
    for n in range(tm // CHUNK):
        cols = []
        for g in range(SGU_GROUPS):
            vb = v[n * CHUNK:(n + 1) * CHUNK, g * GROUP_DIM:(g + 1) * GROUP_DIM]
            mg = jnp.dot(ws_ref[g], vb, preferred_element_type=jnp.float32)
            cols.append(mg + bs_ref[g])
        mixed_rows.append(jnp.concatenate(cols, axis=1))
    mixed = jnp.concatenate(mixed_rows, axis=0)
    hb = jax.nn.gelu(proj(3)) * mixed * jax.nn.silu(proj(5))
    y_b = jnp.dot(hb.astype(jnp.bfloat16), sproj_ref[...], preferred_element_type=jnp.float32)
    m = m + jax.nn.sigmoid(proj(7)) * y_b

    o = jnp.dot(m.astype(jnp.bfloat16), wout_ref[...], preferred_element_type=jnp.float32)
    o = o * lax.rsqrt(jnp.mean(o * o, axis=-1, keepdims=True) + RMS_EPS) * gpost_ref[...]
    o_ref[...] = x_ref[...] + gate * o


def _const_spec(shape):
    zeros = (0,) * len(shape)
    return pl.BlockSpec(shape, lambda b, s: zeros, pipeline_mode=pl.Buffered(1))


def _encoder_layer(x, mod, g_pre, w_in, conv_w, conv_b, cln_g, cln_b, cproj, sln_g, sln_b,
                   ws, bs, sproj, w_out, g_post):
    bsz, seq, d = x.shape
    tm = ROW_TILE
    assert seq % tm == 0 and tm % CHUNK == 0 and d == D_MODEL
    n_s = seq // tm
    halo_per_tile = tm // HALO
    n_halo_blocks = seq // HALO

    in_specs = [
        pl.BlockSpec((None, tm, d), lambda b, s: (b, s, 0)),
        pl.BlockSpec((None, HALO, d), lambda b, s: (b, jnp.maximum(s * halo_per_tile - 1, 0), 0)),
        pl.BlockSpec((None, HALO, d),
                     lambda b, s: (b, jnp.minimum((s + 1) * halo_per_tile, n_halo_blocks - 1), 0)),
        pl.BlockSpec((None, 3, d), lambda b, s: (b, 0, 0)),
        _const_spec(g_pre.shape), _const_spec(w_in.shape), _const_spec(conv_w.shape),
        _const_spec(conv_b.shape), _const_spec(cln_g.shape), _const_spec(cln_b.shape),
        _const_spec(cproj.shape), _const_spec(sln_g.shape), _const_spec(sln_b.shape),
        _const_spec(ws.shape), _const_spec(bs.shape), _const_spec(sproj.shape),
        _const_spec(w_out.shape), _const_spec(g_post.shape),
    ]
    return pl.pallas_call(
        _layer_kernel,
        grid=(bsz, n_s),
        in_specs=in_specs,
        out_specs=pl.BlockSpec((None, tm, d), lambda b, s: (b, s, 0)),
        out_shape=jax.ShapeDtypeStruct(x.shape, x.dtype),
        scratch_shapes=[
            pltpu.VMEM((tm + 2 * HALO, d), jnp.bfloat16),
            pltpu.VMEM((d // LANES, tm + 2 * HALO, LANES), jnp.float32),
            pltpu.VMEM((d // LANES, tm, LANES), jnp.float32),
        ],
        compiler_params=pltpu.CompilerParams(
            dimension_semantics=("arbitrary", "arbitrary"),
            vmem_limit_bytes=VMEM_LIMIT_BYTES),
        name="encoder_layer",
    )(x, x, x, mod, g_pre, w_in, conv_w, conv_b, cln_g, cln_b, cproj, sln_g, sln_b,
      ws, bs, sproj, w_out, g_post)


def kernel(x_prompt, x_sample, c_prompt, c_sample, w_ada, b_ada, g_pre, w_in, conv_w, conv_b,
           conv_ln_g, conv_ln_b, conv_proj, sgu_ln_g, sgu_ln_b, sgu_ws, sgu_bs, sgu_proj, w_out,
           g_post):
    depth = w_ada.shape[0]
    d = D_MODEL
    bp = c_prompt.shape[0]
    mod = _modulation(jnp.concatenate([c_prompt, c_sample], axis=0), w_ada, b_ada)
    mod = mod.reshape(depth, mod.shape[1], 3, d)

    bf16 = jnp.bfloat16
    w_in_b = w_in.astype(bf16)
    cproj_b = conv_proj.astype(bf16)
    sproj_b = sgu_proj.astype(bf16)
    w_out_b = w_out.astype(bf16)
    ws_b = sgu_ws.astype(bf16)
    row = lambda a, l: a[l].reshape(1, d)

    xs = [x_prompt, x_sample]
    mods = [mod[:, :bp], mod[:, bp:]]
    for l in range(depth):
        for i in range(2):
            xs[i] = _encoder_layer(
                xs[i], mods[i][l], row(g_pre, l), w_in_b[l], conv_w[l], row(conv_b, l),
                row(conv_ln_g, l), row(conv_ln_b, l), cproj_b[l], row(sgu_ln_g, l),
                row(sgu_ln_b, l), ws_b[l], sgu_bs[l].reshape(SGU_GROUPS, CHUNK, 1), sproj_b[l],
                w_out_b[l], row(g_post, l))
    return (xs[0], xs[1])
```

```python
import jax
import jax.numpy as jnp
from jax import lax
from jax.experimental import pallas as pl
from jax.experimental.pallas import tpu as pltpu

D_MODEL = 1024
CONV_KERNEL = 31
CONV_PAD = CONV_KERNEL // 2
SGU_GROUPS = 8
GROUP_DIM = D_MODEL // SGU_GROUPS
CHUNK = 128
RMS_EPS = 1e-6
LN_EPS = 1e-5

HALO = 16
ROW_TILE = 512
LANES = 128
SUBLANES = 8
CONV_STRIDE = 4
CONV_ROWS = SUBLANES * CONV_STRIDE
SLICE_WIDTH = 256
CONV_GROUP = 4
VMEM_LIMIT_BYTES = 56 * 1024 * 1024


def _mod_kernel(c_ref, w_ref, b_ref, o_ref):
    c = c_ref[...]
    o_ref[...] = jnp.dot(jax.nn.silu(c), w_ref[...], preferred_element_type=jnp.float32) + b_ref[...]


def _modulation(c_all, w_ada, b_ada):
    depth, d, d3 = w_ada.shape
    r = c_all.shape[0]
    n_col = d3 // d
    return pl.pallas_call(
        _mod_kernel,
        grid=(depth, n_col),
        in_specs=[
            pl.BlockSpec((r, d), lambda l, j: (0, 0)),
            pl.BlockSpec((None, d, d), lambda l, j: (l, 0, j)),
            pl.BlockSpec((None, 1, d), lambda l, j: (l, 0, j)),
        ],
        out_specs=pl.BlockSpec((None, r, d), lambda l, j: (l, 0, j)),
        out_shape=jax.ShapeDtypeStruct((depth, r, d3), jnp.float32),
        name="adaln_mod",
    )(c_all, w_ada, b_ada.reshape(depth, 1, d3))


def _zero_from(x):
    u = pltpu.bitcast(x, jnp.uint32)
    u = lax.shift_right_logical(lax.shift_right_logical(u, jnp.uint32(16)), jnp.uint32(16))
    return pltpu.bitcast(u, jnp.float32)


def _layer_norm(x, g, b):
    mu = jnp.mean(x, axis=-1, keepdims=True)
    xc = x - mu
    var = jnp.mean(xc * xc, axis=-1, keepdims=True)
    return xc * lax.rsqrt(var + LN_EPS) * g + b


def _layer_kernel(x_ref, xp_ref, xn_ref, mod_ref, gpre_ref, win_ref, convw_ref, convb_ref,
                  clng_ref, clnb_ref, cproj_ref, slng_ref, slnb_ref, ws_ref, bs_ref,
                  sproj_ref, wout_ref, gpost_ref, o_ref, h_ref, hs_ref, conv_ref):
    tm = x_ref.shape[0]
    d = D_MODEL
    s_idx = pl.program_id(1)
    n_s = pl.num_programs(1)

    shift = mod_ref[0:1, :]
    scale1 = 1.0 + mod_ref[1:2, :]
    gate = mod_ref[2:3, :]
    g_pre = gpre_ref[...]

    def pre(xv):
        y = xv * lax.rsqrt(jnp.mean(xv * xv, axis=-1, keepdims=True) + RMS_EPS)
        return ((y * g_pre) * scale1 + shift).astype(jnp.bfloat16)

    h_ref[0:HALO, :] = pre(xp_ref[...])
    h_ref[HALO:HALO + tm, :] = pre(x_ref[...])
    h_ref[HALO + tm:, :] = pre(xn_ref[...])

    def proj(col, lo, width, halo=False):
        rows = slice(0, tm + 2 * HALO) if halo else slice(HALO, HALO + tm)
        c0 = col * d + lo
        return jnp.dot(h_ref[rows, :], win_ref[:, c0:c0 + width], preferred_element_type=jnp.float32)

    row = lax.broadcasted_iota(jnp.int32, (tm + 2 * HALO, 1), 0)
    valid = jnp.logical_and(jnp.logical_or(row >= HALO, s_idx > 0),
                            jnp.logical_or(row < HALO + tm, s_idx < n_s - 1))
    n_slices = d // SLICE_WIDTH

    def glu_slice(qi, a_val, a_glu):
        hg = jnp.where(valid, a_val * jax.nn.sigmoid(a_glu), 0.0)
        for i in range(SLICE_WIDTH // LANES):
            hs_ref[qi * (SLICE_WIDTH // LANES) + i] = hg[:, i * LANES:(i + 1) * LANES]

    def conv_blocks(lb, bases, anchor=None):
        w_taps = [jnp.broadcast_to(convw_ref[j:j + 1, lb * LANES:(lb + 1) * LANES], (SUBLANES, LANES))
                  for j in range(CONV_KERNEL)]
        for base in bases:
            acc = [None] * CONV_STRIDE
            for off in range(CONV_STRIDE + CONV_KERNEL - 1):
                e = hs_ref[lb, pl.ds(base + HALO - CONV_PAD + off, SUBLANES, stride=CONV_STRIDE), :]
                for q in range(CONV_STRIDE):
                    j = off - q
                    if 0 <= j < CONV_KERNEL:
                        term = w_taps[j] * e
                        if acc[q] is None:
                            acc[q] = term if anchor is None else term + anchor
                            anchor = None
                        else:
                            acc[q] = acc[q] + term
            for q in range(CONV_STRIDE):
                conv_ref[lb, pl.ds(base + q, SUBLANES, stride=CONV_STRIDE), :] = acc[q]

    glu_in = {}
    anchors = []
    side = {c: [] for c in (4, 3, 5, 2, 6, 7)}
    mxu_tasks = [(c, qi) for qi in range(1, n_slices) for c in (0, 1)]
    mxu_tasks += [(c, qi) for c in side for qi in range(n_slices)]

    def run_mxu_task(c, qi):
        if c in (0, 1):
            glu_in[(c, qi)] = proj(c, qi * SLICE_WIDTH, SLICE_WIDTH, halo=True)
            if (0, qi) in glu_in and (1, qi) in glu_in:
                glu_slice(qi, glu_in[(0, qi)], glu_in[(1, qi)])
        else:
            r = proj(c, qi * SLICE_WIDTH, SLICE_WIDTH)
            side[c].append(r)
            anchors.append(_zero_from(r[0:SUBLANES, 0:LANES]))

    run_mxu_task(0, 0)
    run_mxu_task(1, 0)
    group = CONV_GROUP * CONV_ROWS
    for lb in range(d // LANES):
        for g0 in range(0, tm, group):
            if mxu_tasks:
                run_mxu_task(*mxu_tasks.pop(0))
            conv_blocks(lb, range(g0, g0 + group, CONV_ROWS), anchors.pop(0) if anchors else None)
    while mxu_tasks:
        run_mxu_task(*mxu_tasks.pop(0))
    cat = lambda parts: jnp.concatenate(parts, axis=1)
    p_v, p_u, p_bz, p_az, p_ga, p_gb = (side[c] for c in (4, 3, 5, 2, 6, 7))

    v = _layer_norm(jax.nn.gelu(cat(p_v)), slng_ref[...], slnb_ref[...]).astype(jnp.bfloat16)
    hc = cat([conv_ref[lb] for lb in range(d // LANES)]) + convb_ref[...]
    hc = _layer_norm(hc, clng_ref[...], clnb_ref[...])
    hc = jax.nn.silu(hc) * jax.nn.silu(cat(p_az))
    y_a = jnp.dot(hc.astype(jnp.bfloat16), cproj_ref[...], preferred_element_type=jnp.float32)

    mixed_rows = []
    for n in range(tm // CHUNK):
        cols = []
        for g in range(SGU_GROUPS):
            vb = v[n * CHUNK:(n + 1) * CHUNK, g * GROUP_DIM:(g + 1) * GROUP_DIM]
            mg = jnp.dot(ws_ref[g], vb, preferred_element_type=jnp.float32)
            cols.append(mg + bs_ref[g])
        mixed_rows.append(jnp.concatenate(cols, axis=1))
    mixed = jnp.concatenate(mixed_rows, axis=0)
    hb = jax.nn.gelu(cat(p_u)) * mixed * jax.nn.silu(cat(p_bz))
    y_b = jnp.dot(hb.astype(jnp.bfloat16), sproj_ref[...], preferred_element_type=jnp.float32)
    m = jax.nn.sigmoid(cat(p_ga)) * y_a + jax.nn.sigmoid(cat(p_gb)) * y_b

    o = jnp.dot(m.astype(jnp.bfloat16), wout_ref[...], preferred_element_type=jnp.float32)
    o = o * lax.rsqrt(jnp.mean(o * o, axis=-1, keepdims=True) + RMS_EPS) * gpost_ref[...]
    o_ref[...] = x_ref[...] + gate * o


def _const_spec(shape):
    zeros = (0,) * len(shape)
    return pl.BlockSpec(shape, lambda b, s: zeros, pipeline_mode=pl.Buffered(1))


def _encoder_layer(x, mod, g_pre, w_in, conv_w, conv_b, cln_g, cln_b, cproj, sln_g, sln_b,
                   ws, bs, sproj, w_out, g_post):
    bsz, seq, d = x.shape
    tm = ROW_TILE
    assert seq % tm == 0 and tm % CHUNK == 0 and d == D_MODEL
    n_s = seq // tm
    halo_per_tile = tm // HALO
    n_halo_blocks = seq // HALO

    in_specs = [
        pl.BlockSpec((None, tm, d), lambda b, s: (b, s, 0)),
        pl.BlockSpec((None, HALO, d), lambda b, s: (b, jnp.maximum(s * halo_per_tile - 1, 0), 0)),
        pl.BlockSpec((None, HALO, d),
                     lambda b, s: (b, jnp.minimum((s + 1) * halo_per_tile, n_halo_blocks - 1), 0)),
        pl.BlockSpec((None, 3, d), lambda b, s: (b, 0, 0)),
        _const_spec(g_pre.shape), _const_spec(w_in.shape), _const_spec(conv_w.shape),
        _const_spec(conv_b.shape), _const_spec(cln_g.shape), _const_spec(cln_b.shape),
        _const_spec(cproj.shape), _const_spec(sln_g.shape), _const_spec(sln_b.shape),
        _const_spec(ws.shape), _const_spec(bs.shape), _const_spec(sproj.shape),
        _const_spec(w_out.shape), _const_spec(g_post.shape),
    ]
    return pl.pallas_call(
        _layer_kernel,
        grid=(bsz, n_s),
        in_specs=in_specs,
        out_specs=pl.BlockSpec((None, tm, d), lambda b, s: (b, s, 0)),
        out_shape=jax.ShapeDtypeStruct(x.shape, x.dtype),
        scratch_shapes=[
            pltpu.VMEM((tm + 2 * HALO, d), jnp.bfloat16),
            pltpu.VMEM((d // LANES, tm + 2 * HALO, LANES), jnp.float32),
            pltpu.VMEM((d // LANES, tm, LANES), jnp.float32),
        ],
        compiler_params=pltpu.CompilerParams(
            dimension_semantics=("arbitrary", "arbitrary"),
            vmem_limit_bytes=VMEM_LIMIT_BYTES),
        name="encoder_layer",
    )(x, x, x, mod, g_pre, w_in, conv_w, conv_b, cln_g, cln_b, cproj, sln_g, sln_b,
      ws, bs, sproj, w_out, g_post)


def kernel(x_prompt, x_sample, c_prompt, c_sample, w_ada, b_ada, g_pre, w_in, conv_w, conv_b,
           conv_ln_g, conv_ln_b, conv_proj, sgu_ln_g, sgu_ln_b, sgu_ws, sgu_bs, sgu_proj, w_out,
           g_post):
    depth = w_ada.shape[0]
    d = D_MODEL
    bp = c_prompt.shape[0]
    mod = _modulation(jnp.concatenate([c_prompt, c_sample], axis=0), w_ada, b_ada)
    mod = mod.reshape(depth, mod.shape[1], 3, d)

    bf16 = jnp.bfloat16
    w_in_b = w_in.astype(bf16)
    cproj_b = conv_proj.astype(bf16)
    sproj_b = sgu_proj.astype(bf16)
    w_out_b = w_out.astype(bf16)
    ws_b = sgu_ws.astype(bf16)
    row = lambda a, l: a[l].reshape(1, d)

    xs = [x_prompt, x_sample]
    mods = [mod[:, :bp], mod[:, bp:]]
    for l in range(depth):
        for i in range(2):
            xs[i] = _encoder_layer(
                xs[i], mods[i][l], row(g_pre, l), w_in_b[l], conv_w[l], row(conv_b, l),
                row(conv_ln_g, l), row(conv_ln_b, l), cproj_b[l], row(sgu_ln_g, l),
                row(sgu_ln_b, l), ws_b[l], sgu_bs[l].reshape(SGU_GROUPS, CHUNK, 1), sproj_b[l],
                w_out_b[l], row(g_post, l))
    return (xs[0], xs[1])
```

```python
import jax
import jax.numpy as jnp
from jax import lax
from jax.experimental import pallas as pl
from jax.experimental.pallas import tpu as pltpu

D_MODEL = 1024
CONV_KERNEL = 31
CONV_PAD = CONV_KERNEL // 2
SGU_GROUPS = 8
GROUP_DIM = D_MODEL // SGU_GROUPS
CHUNK = 128
RMS_EPS = 1e-6
LN_EPS = 1e-5

HALO = 16
ROW_TILE = 512
LANES = 128
SUBLANES = 8
CONV_STRIDE = 4
CONV_ROWS = SUBLANES * CONV_STRIDE
SLICE_WIDTH = 256
CONV_GROUP = 4
VMEM_LIMIT_BYTES = 56 * 1024 * 1024


def _mod_kernel(c_ref, w_ref, b_ref, o_ref):
    c = c_ref[...]
    o_ref[...] = jnp.dot(jax.nn.silu(c), w_ref[...], preferred_element_type=jnp.float32) + b_ref[...]


def _modulation(c_all, w_ada, b_ada):
    depth, d, d3 = w_ada.shape
    r = c_all.shape[0]
    n_col = d3 // d
    return pl.pallas_call(
        _mod_kernel,
        grid=(depth, n_col),
        in_specs=[
            pl.BlockSpec((r, d), lambda l, j: (0, 0)),
            pl.BlockSpec((None, d, d), lambda l, j: (l, 0, j)),
            pl.BlockSpec((None, 1, d), lambda l, j: (l, 0, j)),
        ],
        out_specs=pl.BlockSpec((None, r, d), lambda l, j: (l, 0, j)),
        out_shape=jax.ShapeDtypeStruct((depth, r, d3), jnp.float32),
        name="adaln_mod",
    )(c_all, w_ada, b_ada.reshape(depth, 1, d3))


def _zero_from(x):
    u = pltpu.bitcast(x, jnp.uint32)
    u = lax.shift_right_logical(lax.shift_right_logical(u, jnp.uint32(16)), jnp.uint32(16))
    return pltpu.bitcast(u, jnp.float32)


def _gelu(x):
    c = 0.7978845608028654
    hx = 0.5 * x
    return hx * jnp.tanh(x * (c + (c * 0.044715) * (x * x))) + hx


def _layer_norm(x, g, b):
    mu = jnp.mean(x, axis=-1, keepdims=True)
    xc = x - mu
    var = jnp.mean(xc * xc, axis=-1, keepdims=True)
    return xc * lax.rsqrt(var + LN_EPS) * g + b


def _layer_kernel(x_ref, xp_ref, xn_ref, mod_ref, gpre_ref, win_ref, convw_ref, convb_ref,
                  clng_ref, clnb_ref, cproj_ref, slng_ref, slnb_ref, ws_ref, bs_ref,
                  sproj_ref, wout_ref, gpost_ref, o_ref, h_ref, hs_ref, conv_ref):
    tm = x_ref.shape[0]
    d = D_MODEL
    s_idx = pl.program_id(1)
    n_s = pl.num_programs(1)

    shift = mod_ref[0:1, :]
    pre_gain = gpre_ref[...] * (1.0 + mod_ref[1:2, :])
    post_gain = gpost_ref[...] * mod_ref[2:3, :]

    def pre(xv):
        y = xv * lax.rsqrt(jnp.mean(xv * xv, axis=-1, keepdims=True) + RMS_EPS)
        return (y * pre_gain + shift).astype(jnp.bfloat16)

    h_ref[0:HALO, :] = pre(xp_ref[...])
    h_ref[HALO:HALO + tm, :] = pre(x_ref[...])
    h_ref[HALO + tm:, :] = pre(xn_ref[...])

    def proj(col, lo, width, halo=False):
        rows = slice(0, tm + 2 * HALO) if halo else slice(HALO, HALO + tm)
        c0 = col * d + lo
        return jnp.dot(h_ref[rows, :], win_ref[:, c0:c0 + width], preferred_element_type=jnp.float32)

    row = lax.broadcasted_iota(jnp.int32, (tm + 2 * HALO, 1), 0)
    valid = jnp.logical_and(jnp.logical_or(row >= HALO, s_idx > 0),
                            jnp.logical_or(row < HALO + tm, s_idx < n_s - 1))
    n_slices = d // SLICE_WIDTH

    def glu_slice(qi, a_val, a_glu):
        hg = jnp.where(valid, a_val * jax.nn.sigmoid(a_glu), 0.0)
        for i in range(SLICE_WIDTH // LANES):
            hs_ref[qi * (SLICE_WIDTH // LANES) + i] = hg[:, i * LANES:(i + 1) * LANES]

    def conv_blocks(lb, bases, anchor=None):
        w_taps = [jnp.broadcast_to(convw_ref[j:j + 1, lb * LANES:(lb + 1) * LANES], (SUBLANES, LANES))
                  for j in range(CONV_KERNEL)]
        for base in bases:
            acc = [None] * CONV_STRIDE
            for off in range(CONV_STRIDE + CONV_KERNEL - 1):
                e = hs_ref[lb, pl.ds(base + HALO - CONV_PAD + off, SUBLANES, stride=CONV_STRIDE), :]
                for q in range(CONV_STRIDE):
                    j = off - q
                    if 0 <= j < CONV_KERNEL:
                        term = w_taps[j] * e
                        if acc[q] is None:
                            acc[q] = term if anchor is None else term + anchor
                            anchor = None
                        else:
                            acc[q] = acc[q] + term
            for q in range(CONV_STRIDE):
                conv_ref[lb, pl.ds(base + q, SUBLANES, stride=CONV_STRIDE), :] = acc[q]

    glu_in = {}
    anchors = []
    side = {c: [] for c in (4, 3, 5, 2, 6, 7)}
    mxu_tasks = [(c, qi) for qi in range(1, n_slices) for c in (0, 1)]
    mxu_tasks += [(c, qi) for c in side for qi in range(n_slices)]

    def run_mxu_task(c, qi):
        if c in (0, 1):
            glu_in[(c, qi)] = proj(c, qi * SLICE_WIDTH, SLICE_WIDTH, halo=True)
            if (0, qi) in glu_in and (1, qi) in glu_in:
                glu_slice(qi, glu_in[(0, qi)], glu_in[(1, qi)])
        else:
            r = proj(c, qi * SLICE_WIDTH, SLICE_WIDTH)
            side[c].append(r)
            anchors.append(_zero_from(r[0:SUBLANES, 0:LANES]))

    run_mxu_task(0, 0)
    run_mxu_task(1, 0)
    group = CONV_GROUP * CONV_ROWS
    for lb in range(d // LANES):
        for g0 in range(0, tm, group):
            if mxu_tasks:
                run_mxu_task(*mxu_tasks.pop(0))
            conv_blocks(lb, range(g0, g0 + group, CONV_ROWS), anchors.pop(0) if anchors else None)
    while mxu_tasks:
        run_mxu_task(*mxu_tasks.pop(0))
    cat = lambda parts: jnp.concatenate(parts, axis=1)
    p_v, p_u, p_bz, p_az, p_ga, p_gb = (side[c] for c in (4, 3, 5, 2, 6, 7))

    v = _layer_norm(_gelu(cat(p_v)), slng_ref[...], slnb_ref[...]).astype(jnp.bfloat16)
    hc = cat([conv_ref[lb] for lb in range(d // LANES)]) + convb_ref[...]
    hc = _layer_norm(hc, clng_ref[...], clnb_ref[...])
    hc = jax.nn.silu(hc) * jax.nn.silu(cat(p_az))
    y_a = jnp.dot(hc.astype(jnp.bfloat16), cproj_ref[...], preferred_element_type=jnp.float32)

    mixed_rows = []
    for n in range(tm // CHUNK):
        cols = []
        for g in range(SGU_GROUPS):
            vb = v[n * CHUNK:(n + 1) * CHUNK, g * GROUP_DIM:(g + 1) * GROUP_DIM]
            mg = jnp.dot(ws_ref[g], vb, preferred_element_type=jnp.float32)
            cols.append(mg + bs_ref[g])
        mixed_rows.append(jnp.concatenate(cols, axis=1))
    mixed = jnp.concatenate(mixed_rows, axis=0)
    hb = _gelu(cat(p_u)) * mixed * jax.nn.silu(cat(p_bz))
    y_b = jnp.dot(hb.astype(jnp.bfloat16), sproj_ref[...], preferred_element_type=jnp.float32)
    m = jax.nn.sigmoid(cat(p_ga)) * y_a + jax.nn.sigmoid(cat(p_gb)) * y_b

    o = jnp.dot(m.astype(jnp.bfloat16), wout_ref[...], preferred_element_type=jnp.float32)
    o = o * lax.rsqrt(jnp.mean(o * o, axis=-1, keepdims=True) + RMS_EPS)
    o_ref[...] = x_ref[...] + o * post_gain


def _layer_spec(arr, layer):
    tail = arr.shape[1:]
    idx = (layer,) + (0,) * len(tail)
    return pl.BlockSpec((None,) + tail, lambda b, s: idx, pipeline_mode=pl.Buffered(1))


def _encoder_layer(x, layer, mod, mod_row0, params):
    bsz, seq, d = x.shape
    tm = ROW_TILE
    assert seq % tm == 0 and tm % CHUNK == 0 and d == D_MODEL
    n_s = seq // tm
    halo_per_tile = tm // HALO
    n_halo_blocks = seq // HALO

    in_specs = [
        pl.BlockSpec((None, tm, d), lambda b, s: (b, s, 0)),
        pl.BlockSpec((None, HALO, d), lambda b, s: (b, jnp.maximum(s * halo_per_tile - 1, 0), 0)),
        pl.BlockSpec((None, HALO, d),
                     lambda b, s: (b, jnp.minimum((s + 1) * halo_per_tile, n_halo_blocks - 1), 0)),
        pl.BlockSpec((None, None, 3, d), lambda b, s: (layer, mod_row0 + b, 0, 0)),
    ] + [_layer_spec(p, layer) for p in params]
    return pl.pallas_call(
        _layer_kernel,
        grid=(bsz, n_s),
        in_specs=in_specs,
        out_specs=pl.BlockSpec((None, tm, d), lambda b, s: (b, s, 0)),
        out_shape=jax.ShapeDtypeStruct(x.shape, x.dtype),
        scratch_shapes=[
            pltpu.VMEM((tm + 2 * HALO, d), jnp.bfloat16),
            pltpu.VMEM((d // LANES, tm + 2 * HALO, LANES), jnp.float32),
            pltpu.VMEM((d // LANES, tm, LANES), jnp.float32),
        ],
        compiler_params=pltpu.CompilerParams(
            dimension_semantics=("arbitrary", "arbitrary"),
            vmem_limit_bytes=VMEM_LIMIT_BYTES),
        name="encoder_layer",
    )(x, x, x, mod, *params)


def kernel(x_prompt, x_sample, c_prompt, c_sample, w_ada, b_ada, g_pre, w_in, conv_w, conv_b,
           conv_ln_g, conv_ln_b, conv_proj, sgu_ln_g, sgu_ln_b, sgu_ws, sgu_bs, sgu_proj, w_out,
           g_post):
    depth = w_ada.shape[0]
    d = D_MODEL
    bp = c_prompt.shape[0]
    mod = _modulation(jnp.concatenate([c_prompt, c_sample], axis=0), w_ada, b_ada)
    mod = mod.reshape(depth, mod.shape[1], 3, d)

    bf16 = jnp.bfloat16
    rows = lambda a: a.reshape(depth, 1, d)
    params = (rows(g_pre), w_in.astype(bf16), conv_w, rows(conv_b), rows(conv_ln_g), rows(conv_ln_b),
              conv_proj.astype(bf16), rows(sgu_ln_g), rows(sgu_ln_b), sgu_ws.astype(bf16),
              sgu_bs.reshape(depth, SGU_GROUPS, CHUNK, 1), sgu_proj.astype(bf16), w_out.astype(bf16),
              rows(g_post))

    y_prompt, y_sample = x_prompt, x_sample
    for layer in range(depth):
        y_prompt = _encoder_layer(y_prompt, layer, mod, 0, params)
        y_sample = _encoder_layer(y_sample, layer, mod, bp, params)
    return (y_prompt, y_sample)
```

```python
import jax
import jax.numpy as jnp
from jax import lax
from jax.experimental import pallas as pl
from jax.experimental.pallas import tpu as pltpu

D_MODEL = 1024
CONV_KERNEL = 31
CONV_PAD = CONV_KERNEL // 2
SGU_GROUPS = 8
GROUP_DIM = D_MODEL // SGU_GROUPS
CHUNK = 128
RMS_EPS = 1e-6
LN_EPS = 1e-5

HALO = 16
ROW_TILE = 512
LANES = 128
SUBLANES = 8
CONV_STRIDE = 4
CONV_ROWS = SUBLANES * CONV_STRIDE
SLICE_WIDTH = 256
CONV_GROUP = 4
VMEM_LIMIT_BYTES = 56 * 1024 * 1024


def _mod_kernel(c_ref, w_ref, b_ref, o_ref):
    c = c_ref[...]
    o_ref[...] = jnp.dot(jax.nn.silu(c), w_ref[...], preferred_element_type=jnp.float32) + b_ref[...]


def _modulation(c_all, w_ada, b_ada):
    depth, d, d3 = w_ada.shape
    r = c_all.shape[0]
    n_col = d3 // d
    return pl.pallas_call(
        _mod_kernel,
        grid=(depth, n_col),
        in_specs=[
            pl.BlockSpec((r, d), lambda l, j: (0, 0)),
            pl.BlockSpec((None, d, d), lambda l, j: (l, 0, j)),
            pl.BlockSpec((None, 1, d), lambda l, j: (l, 0, j)),
        ],
        out_specs=pl.BlockSpec((None, r, d), lambda l, j: (l, 0, j)),
        out_shape=jax.ShapeDtypeStruct((depth, r, d3), jnp.float32),
        name="adaln_mod",
    )(c_all, w_ada, b_ada.reshape(depth, 1, d3))


def _zero_from(x):
    u = pltpu.bitcast(x, jnp.uint32)
    u = lax.shift_right_logical(lax.shift_right_logical(u, jnp.uint32(16)), jnp.uint32(16))
    return pltpu.bitcast(u, jnp.float32)


def _gelu(x):
    c = 0.7978845608028654
    hx = 0.5 * x
    return hx * jnp.tanh(x * (c + (c * 0.044715) * (x * x))) + hx


def _layer_norm(x, g, b):
    mu = jnp.mean(x, axis=-1, keepdims=True)
    xc = x - mu
    var = jnp.mean(xc * xc, axis=-1, keepdims=True)
    return xc * lax.rsqrt(var + LN_EPS) * g + b


def _layer_kernel(x_ref, xp_ref, xn_ref, mod_ref, gpre_ref, win_ref, convw_ref, convb_ref,
                  clng_ref, clnb_ref, cproj_ref, slng_ref, slnb_ref, ws_ref, bs_ref,
                  sproj_ref, wout_ref, gpost_ref, o_ref, h_ref, hs_ref, conv_ref):
    tm = x_ref.shape[0]
    d = D_MODEL
    s_idx = pl.program_id(1)
    n_s = pl.num_programs(1)

    shift = mod_ref[0:1, :]
    pre_gain = gpre_ref[...] * (1.0 + mod_ref[1:2, :])
    post_gain = gpost_ref[...] * mod_ref[2:3, :]

    def pre(xv):
        y = xv * lax.rsqrt(jnp.mean(xv * xv, axis=-1, keepdims=True) + RMS_EPS)
        return (y * pre_gain + shift).astype(jnp.bfloat16)

    h_ref[0:HALO, :] = pre(xp_ref[...])
    h_ref[HALO:HALO + tm, :] = pre(x_ref[...])
    h_ref[HALO + tm:, :] = pre(xn_ref[...])

    def proj(col, lo, width, halo=False):
        rows = slice(0, tm + 2 * HALO) if halo else slice(HALO, HALO + tm)
        c0 = col * d + lo
        return jnp.dot(h_ref[rows, :], win_ref[:, c0:c0 + width], preferred_element_type=jnp.float32)

    row = lax.broadcasted_iota(jnp.int32, (tm + 2 * HALO, 1), 0)
    valid = jnp.logical_and(jnp.logical_or(row >= HALO, s_idx > 0),
                            jnp.logical_or(row < HALO + tm, s_idx < n_s - 1))
    n_slices = d // SLICE_WIDTH

    def glu_slice(qi, a_val, a_glu):
        hg = jnp.where(valid, a_val * jax.nn.sigmoid(a_glu), 0.0)
        for i in range(SLICE_WIDTH // LANES):
            hs_ref[qi * (SLICE_WIDTH // LANES) + i] = hg[:, i * LANES:(i + 1) * LANES]

    def conv_blocks(lb, bases, anchor=None):
        w_taps = [jnp.broadcast_to(convw_ref[j:j + 1, lb * LANES:(lb + 1) * LANES], (SUBLANES, LANES))
                  for j in range(CONV_KERNEL)]
        for base in bases:
            acc = [None] * CONV_STRIDE
            for off in range(CONV_STRIDE + CONV_KERNEL - 1):
                e = hs_ref[lb, pl.ds(base + HALO - CONV_PAD + off, SUBLANES, stride=CONV_STRIDE), :]
                for q in range(CONV_STRIDE):
                    j = off - q
                    if 0 <= j < CONV_KERNEL:
                        term = w_taps[j] * e
                        if acc[q] is None:
                            acc[q] = term if anchor is None else term + anchor
                            anchor = None
                        else:
                            acc[q] = acc[q] + term
            for q in range(CONV_STRIDE):
                conv_ref[lb, pl.ds(base + q, SUBLANES, stride=CONV_STRIDE), :] = acc[q]

    glu_in = {}
    anchors = []
    side = {c: [] for c in (4, 3, 5, 2, 6, 7)}
    mxu_tasks = [(c, qi) for qi in range(1, n_slices) for c in (0, 1)]
    mxu_tasks += [(c, qi) for c in side for qi in range(n_slices)]

    def run_mxu_task(c, qi):
        if c in (0, 1):
            glu_in[(c, qi)] = proj(c, qi * SLICE_WIDTH, SLICE_WIDTH, halo=True)
            if (0, qi) in glu_in and (1, qi) in glu_in:
                glu_slice(qi, glu_in[(0, qi)], glu_in[(1, qi)])
        else:
            r = proj(c, qi * SLICE_WIDTH, SLICE_WIDTH)
            side[c].append(r)
            anchors.append(_zero_from(r[0:SUBLANES, 0:LANES]))

    run_mxu_task(0, 0)
    run_mxu_task(1, 0)
    group = CONV_GROUP * CONV_ROWS
    for lb in range(d // LANES):
        for g0 in range(0, tm, group):
            if mxu_tasks:
                run_mxu_task(*mxu_tasks.pop(0))
            conv_blocks(lb, range(g0, g0 + group, CONV_ROWS), anchors.pop(0) if anchors else None)
    while mxu_tasks:
        run_mxu_task(*mxu_tasks.pop(0))
    cat = lambda parts: jnp.concatenate(parts, axis=1)
    p_v, p_u, p_bz, p_az, p_ga, p_gb = (side[c] for c in (4, 3, 5, 2, 6, 7))

    v = _layer_norm(_gelu(cat(p_v)), slng_ref[...], slnb_ref[...]).astype(jnp.bfloat16)
    hc = cat([conv_ref[lb] for lb in range(d // LANES)]) + convb_ref[...]
    hc = _layer_norm(hc, clng_ref[...], clnb_ref[...])
    hc = jax.nn.silu(hc).astype(jnp.bfloat16) * jax.nn.silu(cat(p_az).astype(jnp.bfloat16))
    y_a = jnp.dot(hc, cproj_ref[...], preferred_element_type=jnp.float32)

    mixed_rows = []
    for n in range(tm // CHUNK):
        cols = []
        for g in range(SGU_GROUPS):
            vb = v[n * CHUNK:(n + 1) * CHUNK, g * GROUP_DIM:(g + 1) * GROUP_DIM]
            mg = jnp.dot(ws_ref[g], vb, preferred_element_type=jnp.float32)
            cols.append(mg + bs_ref[g])
        mixed_rows.append(jnp.concatenate(cols, axis=1))
    mixed = jnp.concatenate(mixed_rows, axis=0)
    hb = (_gelu(cat(p_u).astype(jnp.bfloat16)) * jax.nn.silu(cat(p_bz).astype(jnp.bfloat16))
          * mixed.astype(jnp.bfloat16))
    y_b = jnp.dot(hb, sproj_ref[...], preferred_element_type=jnp.float32)
    m = (jax.nn.sigmoid(cat(p_ga).astype(jnp.bfloat16)) * y_a.astype(jnp.bfloat16)
         + jax.nn.sigmoid(cat(p_gb).astype(jnp.bfloat16)) * y_b.astype(jnp.bfloat16))

    o = jnp.dot(m, wout_ref[...], preferred_element_type=jnp.float32)
    o = o * lax.rsqrt(jnp.mean(o * o, axis=-1, keepdims=True) + RMS_EPS)
    o_ref[...] = x_ref[...] + o * post_gain


def _layer_spec(arr, layer):
    tail = arr.shape[1:]
    idx = (layer,) + (0,) * len(tail)
    return pl.BlockSpec((None,) + tail, lambda b, s: idx, pipeline_mode=pl.Buffered(1))


def _encoder_layer(x, layer, mod, mod_row0, params):
    bsz, seq, d = x.shape
    tm = ROW_TILE
    assert seq % tm == 0 and tm % CHUNK == 0 and d == D_MODEL
    n_s = seq // tm
    halo_per_tile = tm // HALO
    n_halo_blocks = seq // HALO

    in_specs = [
        pl.BlockSpec((None, tm, d), lambda b, s: (b, s, 0)),
        pl.BlockSpec((None, HALO, d), lambda b, s: (b, jnp.maximum(s * halo_per_tile - 1, 0), 0)),
        pl.BlockSpec((None, HALO, d),
                     lambda b, s: (b, jnp.minimum((s + 1) * halo_per_tile, n_halo_blocks - 1), 0)),
        pl.BlockSpec((None, None, 3, d), lambda b, s: (layer, mod_row0 + b, 0, 0)),
    ] + [_layer_spec(p, layer) for p in params]
    return pl.pallas_call(
        _layer_kernel,
        grid=(bsz, n_s),
        in_specs=in_specs,
        out_specs=pl.BlockSpec((None, tm, d), lambda b, s: (b, s, 0)),
        out_shape=jax.ShapeDtypeStruct(x.shape, x.dtype),
        scratch_shapes=[
            pltpu.VMEM((tm + 2 * HALO, d), jnp.bfloat16),
            pltpu.VMEM((d // LANES, tm + 2 * HALO, LANES), jnp.float32),
            pltpu.VMEM((d // LANES, tm, LANES), jnp.float32),
        ],
        compiler_params=pltpu.CompilerParams(
            dimension_semantics=("arbitrary", "arbitrary"),
            vmem_limit_bytes=VMEM_LIMIT_BYTES),
        name="encoder_layer",
    )(x, x, x, mod, *params)


def kernel(x_prompt, x_sample, c_prompt, c_sample, w_ada, b_ada, g_pre, w_in, conv_w, conv_b,
           conv_ln_g, conv_ln_b, conv_proj, sgu_ln_g, sgu_ln_b, sgu_ws, sgu_bs, sgu_proj, w_out,
           g_post):
    depth = w_ada.shape[0]
    d = D_MODEL
    bp = c_prompt.shape[0]
    mod = _modulation(jnp.concatenate([c_prompt, c_sample], axis=0), w_ada, b_ada)
    mod = mod.reshape(depth, mod.shape[1], 3, d)

    bf16 = jnp.bfloat16
    rows = lambda a: a.reshape(depth, 1, d)
    params = (rows(g_pre), w_in.astype(bf16), conv_w, rows(conv_b), rows(conv_ln_g), rows(conv_ln_b),
              conv_proj.astype(bf16), rows(sgu_ln_g), rows(sgu_ln_b), sgu_ws.astype(bf16),
              sgu_bs.reshape(depth, SGU_GROUPS, CHUNK, 1), sgu_proj.astype(bf16), w_out.astype(bf16),
              rows(g_post))

    y_prompt, y_sample = x_prompt, x_sample
    for layer in range(depth):
        y_prompt = _encoder_layer(y_prompt, layer, mod, 0, params)
        y_sample = _encoder_layer(y_sample, layer, mod, bp, params)
    return (y_prompt, y_sample)
```

```python
import jax
import jax.numpy as jnp
from jax import lax
from jax.experimental import pallas as pl
from jax.experimental.pallas import tpu as pltpu

D_MODEL = 1024
CONV_KERNEL = 31
CONV_PAD = CONV_KERNEL // 2
SGU_GROUPS = 8
GROUP_DIM = D_MODEL // SGU_GROUPS
CHUNK = 128
RMS_EPS = 1e-6
LN_EPS = 1e-5

HALO = 16
ROW_TILE = 512
LANES = 128
SUBLANES = 8
CONV_STRIDE = 4
CONV_ROWS = SUBLANES * CONV_STRIDE
SLICE_WIDTH = 256
CONV_GROUP = 4
VMEM_LIMIT_BYTES = 56 * 1024 * 1024


def _mod_kernel(c_ref, w_ref, b_ref, o_ref):
    c = c_ref[...]
    o_ref[...] = jnp.dot(jax.nn.silu(c), w_ref[...], preferred_element_type=jnp.float32) + b_ref[...]


def _modulation(c_all, w_ada, b_ada):
    depth, d, d3 = w_ada.shape
    r = c_all.shape[0]
    n_col = d3 // d
    return pl.pallas_call(
        _mod_kernel,
        grid=(depth, n_col),
        in_specs=[
            pl.BlockSpec((r, d), lambda l, j: (0, 0)),
            pl.BlockSpec((None, d, d), lambda l, j: (l, 0, j)),
            pl.BlockSpec((None, 1, d), lambda l, j: (l, 0, j)),
        ],
        out_specs=pl.BlockSpec((None, r, d), lambda l, j: (l, 0, j)),
        out_shape=jax.ShapeDtypeStruct((depth, r, d3), jnp.float32),
        name="adaln_mod",
    )(c_all, w_ada, b_ada.reshape(depth, 1, d3))


def _zero_from(x):
    u = pltpu.bitcast(x, jnp.uint32)
    u = lax.shift_right_logical(lax.shift_right_logical(u, jnp.uint32(16)), jnp.uint32(16))
    return pltpu.bitcast(u, jnp.float32)


def _gelu(x):
    c = 0.7978845608028654
    hx = 0.5 * x
    return hx * jnp.tanh(x * (c + (c * 0.044715) * (x * x))) + hx


def _layer_norm(x, g, b):
    mu = jnp.mean(x, axis=-1, keepdims=True)
    xc = x - mu
    var = jnp.mean(xc * xc, axis=-1, keepdims=True)
    return xc * lax.rsqrt(var + LN_EPS) * g + b


def _layer_kernel(x_ref, xp_ref, xn_ref, mod_ref, gpre_ref, win_ref, convw_ref, convb_ref,
                  clng_ref, clnb_ref, cproj_ref, slng_ref, slnb_ref, ws_ref, bs_ref,
                  sproj_ref, wout_ref, gpost_ref, o_ref, h_ref, hs_ref, conv_ref):
    tm = x_ref.shape[0]
    d = D_MODEL
    s_idx = pl.program_id(1)
    n_s = pl.num_programs(1)

    shift = mod_ref[0:1, :]
    pre_gain = gpre_ref[...] * (1.0 + mod_ref[1:2, :])
    post_gain = gpost_ref[...] * mod_ref[2:3, :]

    def pre(xv):
        y = xv * lax.rsqrt(jnp.mean(xv * xv, axis=-1, keepdims=True) + RMS_EPS)
        return (y * pre_gain + shift).astype(jnp.bfloat16)

    h_ref[0:HALO, :] = pre(xp_ref[...])
    h_ref[HALO:HALO + tm, :] = pre(x_ref[...])
    h_ref[HALO + tm:, :] = pre(xn_ref[...])

    def proj(col, lo, width, halo=False):
        rows = slice(0, tm + 2 * HALO) if halo else slice(HALO, HALO + tm)
        c0 = col * d + lo
        return jnp.dot(h_ref[rows, :], win_ref[:, c0:c0 + width], preferred_element_type=jnp.float32)

    row = lax.broadcasted_iota(jnp.int32, (tm + 2 * HALO, 1), 0)
    valid = jnp.logical_and(jnp.logical_or(row >= HALO, s_idx > 0),
                            jnp.logical_or(row < HALO + tm, s_idx < n_s - 1))
    n_slices = d // SLICE_WIDTH

    def glu_slice(qi, a_val, a_glu):
        hg = jnp.where(valid, a_val * jax.nn.sigmoid(a_glu), 0.0)
        for i in range(SLICE_WIDTH // LANES):
            hs_ref[qi * (SLICE_WIDTH // LANES) + i] = hg[:, i * LANES:(i + 1) * LANES]

    def conv_blocks(lb, bases, anchor=None):
        w_taps = [jnp.broadcast_to(convw_ref[j:j + 1, lb * LANES:(lb + 1) * LANES], (SUBLANES, LANES))
                  for j in range(CONV_KERNEL)]
        bias = jnp.broadcast_to(convb_ref[:, lb * LANES:(lb + 1) * LANES], (SUBLANES, LANES))
        for base in bases:
            acc = [None] * CONV_STRIDE
            for off in range(CONV_STRIDE + CONV_KERNEL - 1):
                e = hs_ref[lb, pl.ds(base + HALO - CONV_PAD + off, SUBLANES, stride=CONV_STRIDE), :]
                for q in range(CONV_STRIDE):
                    j = off - q
                    if 0 <= j < CONV_KERNEL:
                        term = w_taps[j] * e
                        if acc[q] is None:
                            acc[q] = term + bias if anchor is None else term + (bias + anchor)
                            anchor = None
                        else:
                            acc[q] = acc[q] + term
            for q in range(CONV_STRIDE):
                conv_ref[lb, pl.ds(base + q, SUBLANES, stride=CONV_STRIDE), :] = acc[q]

    glu_in = {}
    anchors = []
    side = {c: [] for c in (4, 3, 5, 2, 6, 7)}
    mxu_tasks = [(c, qi) for qi in range(1, n_slices) for c in (0, 1)]
    mxu_tasks += [(c, qi) for c in side for qi in range(n_slices)]

    def run_mxu_task(c, qi):
        if c in (0, 1):
            glu_in[(c, qi)] = proj(c, qi * SLICE_WIDTH, SLICE_WIDTH, halo=True)
            if (0, qi) in glu_in and (1, qi) in glu_in:
                glu_slice(qi, glu_in[(0, qi)], glu_in[(1, qi)])
        else:
            r = proj(c, qi * SLICE_WIDTH, SLICE_WIDTH)
            side[c].append(r)
            anchors.append(_zero_from(r[0:SUBLANES, 0:LANES]))

    run_mxu_task(0, 0)
    run_mxu_task(1, 0)
    group = CONV_GROUP * CONV_ROWS
    for lb in range(d // LANES):
        for g0 in range(0, tm, group):
            if mxu_tasks:
                run_mxu_task(*mxu_tasks.pop(0))
            conv_blocks(lb, range(g0, g0 + group, CONV_ROWS), anchors.pop(0) if anchors else None)
    while mxu_tasks:
        run_mxu_task(*mxu_tasks.pop(0))
    cat = lambda parts: jnp.concatenate(parts, axis=1)
    p_v, p_u, p_bz, p_az, p_ga, p_gb = (side[c] for c in (4, 3, 5, 2, 6, 7))

    v = _layer_norm(_gelu(cat(p_v)), slng_ref[...], slnb_ref[...]).astype(jnp.bfloat16)
    hc = cat([conv_ref[lb] for lb in range(d // LANES)])
    hc = _layer_norm(hc, clng_ref[...], clnb_ref[...])
    hc = jax.nn.silu(hc.astype(jnp.bfloat16)) * jax.nn.silu(cat(p_az).astype(jnp.bfloat16))
    y_a = jnp.dot(hc, cproj_ref[...], preferred_element_type=jnp.float32)

    mixed_rows = []
    for n in range(tm // CHUNK):
        cols = []
        for g in range(SGU_GROUPS):
            vb = v[n * CHUNK:(n + 1) * CHUNK, g * GROUP_DIM:(g + 1) * GROUP_DIM]
            mg = jnp.dot(ws_ref[g], vb, preferred_element_type=jnp.float32)
            cols.append(mg + bs_ref[g])
        mixed_rows.append(jnp.concatenate(cols, axis=1))
    mixed = jnp.concatenate(mixed_rows, axis=0)
    hb = (_gelu(cat(p_u).astype(jnp.bfloat16)) * jax.nn.silu(cat(p_bz).astype(jnp.bfloat16))
          * mixed.astype(jnp.bfloat16))
    y_b = jnp.dot(hb, sproj_ref[...], preferred_element_type=jnp.float32)
    m = (jax.nn.sigmoid(cat(p_ga).astype(jnp.bfloat16)) * y_a.astype(jnp.bfloat16)
         + jax.nn.sigmoid(cat(p_gb).astype(jnp.bfloat16)) * y_b.astype(jnp.bfloat16))

    o = jnp.dot(m, wout_ref[...], preferred_element_type=jnp.float32)
    o = o * lax.rsqrt(jnp.mean(o * o, axis=-1, keepdims=True) + RMS_EPS)
    o_ref[...] = x_ref[...] + o * post_gain


def _layer_spec(arr, layer):
    tail = arr.shape[1:]
    idx = (layer,) + (0,) * len(tail)
    return pl.BlockSpec((None,) + tail, lambda b, s: idx, pipeline_mode=pl.Buffered(1))


def _encoder_layer(x, layer, mod, mod_row0, params):
    bsz, seq, d = x.shape
    tm = ROW_TILE
    assert seq % tm == 0 and tm % CHUNK == 0 and d == D_MODEL
    n_s = seq // tm
    halo_per_tile = tm // HALO
    n_halo_blocks = seq // HALO

    in_specs = [
        pl.BlockSpec((None, tm, d), lambda b, s: (b, s, 0)),
        pl.BlockSpec((None, HALO, d), lambda b, s: (b, jnp.maximum(s * halo_per_tile - 1, 0), 0)),
        pl.BlockSpec((None, HALO, d),
                     lambda b, s: (b, jnp.minimum((s + 1) * halo_per_tile, n_halo_blocks - 1), 0)),
        pl.BlockSpec((None, None, 3, d), lambda b, s: (layer, mod_row0 + b, 0, 0)),
    ] + [_layer_spec(p, layer) for p in params]
    return pl.pallas_call(
        _layer_kernel,
        grid=(bsz, n_s),
        in_specs=in_specs,
        out_specs=pl.BlockSpec((None, tm, d), lambda b, s: (b, s, 0)),
        out_shape=jax.ShapeDtypeStruct(x.shape, x.dtype),
        scratch_shapes=[
            pltpu.VMEM((tm + 2 * HALO, d), jnp.bfloat16),
            pltpu.VMEM((d // LANES, tm + 2 * HALO, LANES), jnp.float32),
            pltpu.VMEM((d // LANES, tm, LANES), jnp.float32),
        ],
        compiler_params=pltpu.CompilerParams(
            dimension_semantics=("arbitrary", "arbitrary"),
            vmem_limit_bytes=VMEM_LIMIT_BYTES),
        name="encoder_layer",
    )(x, x, x, mod, *params)


def kernel(x_prompt, x_sample, c_prompt, c_sample, w_ada, b_ada, g_pre, w_in, conv_w, conv_b,
           conv_ln_g, conv_ln_b, conv_proj, sgu_ln_g, sgu_ln_b, sgu_ws, sgu_bs, sgu_proj, w_out,
           g_post):
    depth = w_ada.shape[0]
    d = D_MODEL
    bp = c_prompt.shape[0]
    mod = _modulation(jnp.concatenate([c_prompt, c_sample], axis=0), w_ada, b_ada)
    mod = mod.reshape(depth, mod.shape[1], 3, d)

    bf16 = jnp.bfloat16
    rows = lambda a: a.reshape(depth, 1, d)
    params = (rows(g_pre), w_in.astype(bf16), conv_w, rows(conv_b), rows(conv_ln_g), rows(conv_ln_b),
              conv_proj.astype(bf16), rows(sgu_ln_g), rows(sgu_ln_b), sgu_ws.astype(bf16),
              sgu_bs.reshape(depth, SGU_GROUPS, CHUNK, 1), sgu_proj.astype(bf16), w_out.astype(bf16),
              rows(g_post))

    y_prompt, y_sample = x_prompt, x_sample
    for layer in range(depth):
        y_prompt = _encoder_layer(y_prompt, layer, mod, 0, params)
        y_sample = _encoder_layer(y_sample, layer, mod, bp, params)
    return (y_prompt, y_sample)
```
